```python
import jax, jax.numpy as jnp
from jax import lax
import numpy as np

D_MODEL = 4096
BATCH = 2
SEQ = 4096
DEPTH = 2

HEAD_DIM = 128
N_HEADS = D_MODEL // HEAD_DIM
MIX_HALF = D_MODEL // 2
MOBA_HEADS = MIX_HALF // HEAD_DIM
DIFF_HEADS = MIX_HALF // (2 * HEAD_DIM)
SB_HEADS = N_HEADS
MOBA_BLOCK = 256
MOBA_TOPK = 3
MOBA_Q_CHUNK = 32
Q_BLOCK = 128
ROPE_THETA = 10000.0
N_EXPERTS = 32
TOP_K = 4
D_EXPERT = 768
SWIGLU_LIMIT = 7.0
SWIGLU_ALPHA = 1.702
EXPERT_BLOCK = 256
ADA_RANK = 512
RMS_EPS = 1e-6
NEG_INF = -1e30
N_EVEN = (DEPTH + 1) // 2
N_ODD = DEPTH // 2

kernel_name = 'hybrid_moba_diffattn_stickbreak_moe'


def rmsnorm(x, w):
    xf = x.astype(jnp.float32)
    y = xf * lax.rsqrt(jnp.mean(xf * xf, axis=-1, keepdims=True) + RMS_EPS)
    return (y * w.astype(jnp.float32)).astype(x.dtype)


def rope_tables(seq_len):
    pos = jnp.arange(seq_len, dtype=jnp.float32)
    inv_freq = ROPE_THETA ** (-jnp.arange(0, HEAD_DIM, 2, dtype=jnp.float32) / HEAD_DIM)
    ang = pos[:, None] * inv_freq[None, :]
    return jnp.cos(ang), jnp.sin(ang)


def apply_rope(t, cos, sin):
    tf = t.astype(jnp.float32)
    t1, t2 = jnp.split(tf, 2, axis=-1)
    c = cos[None, :, None, :]
    s = sin[None, :, None, :]
    return jnp.concatenate([t1 * c - t2 * s, t2 * c + t1 * s], axis=-1).astype(t.dtype)


def moba_attention(q, k, v):
    B, S, H, hd = q.shape
    nb = -(-S // MOBA_BLOCK)
    pad = nb * MOBA_BLOCK - S
    topk = min(MOBA_TOPK, nb)
    scale = hd ** -0.5
    qt = q.transpose(0, 2, 1, 3)
    kt = jnp.pad(k.transpose(0, 2, 1, 3), ((0, 0), (0, 0), (0, pad), (0, 0)))
    vt = jnp.pad(v.transpose(0, 2, 1, 3), ((0, 0), (0, 0), (0, pad), (0, 0)))
    kb = kt.reshape(B, H, nb, MOBA_BLOCK, hd)
    vb = vt.reshape(B, H, nb, MOBA_BLOCK, hd)
    kmean = jnp.mean(kb.astype(jnp.float32), axis=3)
    gather_blocks = jax.vmap(jax.vmap(lambda blocks, ix: blocks[ix]))
    blk_ids = jnp.arange(nb)
    in_blk = jnp.arange(MOBA_BLOCK)

    def chunk_fn(ci):
        start = ci * MOBA_Q_CHUNK
        qc = lax.dynamic_slice_in_dim(qt, start, MOBA_Q_CHUNK, axis=2)
        qpos = start + jnp.arange(MOBA_Q_CHUNK)
        qblk = start // MOBA_BLOCK
        gate = jnp.einsum('bhqd,bhnd->bhqn', qc.astype(jnp.float32), kmean)
        gate = jnp.where(blk_ids < qblk, gate, NEG_INF)
        _, sel = lax.top_k(gate, topk)
        sel_valid = sel < qblk
        ks = gather_blocks(kb, sel)
        vs = gather_blocks(vb, sel)
        own_k = lax.dynamic_index_in_dim(kb, qblk, axis=2, keepdims=False)
        own_v = lax.dynamic_index_in_dim(vb, qblk, axis=2, keepdims=False)
        s_sel = jnp.einsum('bhqd,bhqnkd->bhqnk', qc, ks).astype(jnp.float32) * scale
        s_sel = jnp.where(sel_valid[..., None], s_sel, NEG_INF)
        s_sel = s_sel.reshape(B, H, MOBA_Q_CHUNK, topk * MOBA_BLOCK)
        s_own = jnp.einsum('bhqd,bhkd->bhqk', qc, own_k).astype(jnp.float32) * scale
        own_mask = (qblk * MOBA_BLOCK + in_blk)[None, :] <= qpos[:, None]
        s_own = jnp.where(own_mask, s_own, NEG_INF)
        p = jax.nn.softmax(jnp.concatenate([s_sel, s_own], axis=-1), axis=-1)
        p_sel = p[..., :topk * MOBA_BLOCK].reshape(B, H, MOBA_Q_CHUNK, topk, MOBA_BLOCK)
        p_own = p[..., topk * MOBA_BLOCK:]
        out = jnp.einsum('bhqnk,bhqnkd->bhqd', p_sel.astype(v.dtype), vs)
        out = out + jnp.einsum('bhqk,bhkd->bhqd', p_own.astype(v.dtype), own_v)
        return out

    outs = lax.map(chunk_fn, jnp.arange(S // MOBA_Q_CHUNK))
    return outs.transpose(1, 0, 3, 2, 4).reshape(B, S, H, hd)


def diff_attention(q, k, v, lam, subln_w, lambda_init):
    B, S, H, _, hd = q.shape
    scale = hd ** -0.5
    qt = q.transpose(0, 2, 3, 1, 4)
    kt = k.transpose(0, 2, 3, 1, 4)
    vt = v.transpose(0, 2, 1, 3)
    kpos = jnp.arange(S)

    def block_fn(bi):
        start = bi * Q_BLOCK
        qb = lax.dynamic_slice_in_dim(qt, start, Q_BLOCK, axis=3)
        s = jnp.einsum('bhcqd,bhckd->bhcqk', qb, kt).astype(jnp.float32) * scale
        qpos = start + jnp.arange(Q_BLOCK)
        mask = kpos[None, :] <= qpos[:, None]
        p = jax.nn.softmax(jnp.where(mask, s, NEG_INF), axis=-1)
        w = p[:, :, 0] - lam * p[:, :, 1]
        return jnp.einsum('bhqk,bhkd->bhqd', w.astype(v.dtype), vt)

    outs = lax.map(block_fn, jnp.arange(S // Q_BLOCK))
    o = outs.transpose(1, 0, 3, 2, 4).reshape(B, S, H, 2 * hd)
    o = rmsnorm(o, subln_w) * (1.0 - lambda_init)
    return o.reshape(B, S, H * 2 * hd)


def stick_breaking_attention(q, k, v):
    B, S, H, hd = q.shape
    scale = hd ** -0.5
    qt = q.transpose(0, 2, 1, 3)
    kt = k.transpose(0, 2, 1, 3)
    vt = v.transpose(0, 2, 1, 3)
    kpos = jnp.arange(S)

    def block_fn(bi):
        start = bi * Q_BLOCK
        qb = lax.dynamic_slice_in_dim(qt, start, Q_BLOCK, axis=2)
        z = jnp.einsum('bhqd,bhkd->bhqk', qb, kt).astype(jnp.float32) * scale
        qpos = start + jnp.arange(Q_BLOCK)
        strict = kpos[None, :] < qpos[:, None]
        log_beta = jax.nn.log_sigmoid(z)
        log_1m = jnp.where(strict, jax.nn.log_sigmoid(-z), 0.0)
        tail = lax.cumsum(log_1m, axis=3, reverse=True) - log_1m
        a = jnp.where(strict, jnp.exp(log_beta + tail), 0.0)
        return jnp.einsum('bhqk,bhkd->bhqd', a.astype(v.dtype), vt)

    outs = lax.map(block_fn, jnp.arange(S // Q_BLOCK))
    return outs.transpose(1, 0, 3, 2, 4).reshape(B, S, H, hd)


def even_mixer(h, w_in, w_out, lq1, lk1, lq2, lk2, subln_w, lambda_init, cos, sin):
    B, S, _ = h.shape
    proj = h @ w_in
    qa, ka, va, qb, kb, vb = jnp.split(proj, 6, axis=-1)
    qa = apply_rope(qa.reshape(B, S, MOBA_HEADS, HEAD_DIM), cos, sin)
    ka = apply_rope(ka.reshape(B, S, MOBA_HEADS, HEAD_DIM), cos, sin)
    va = va.reshape(B, S, MOBA_HEADS, HEAD_DIM)
    out_a = moba_attention(qa, ka, va).reshape(B, S, MIX_HALF)
    qb = apply_rope(qb.reshape(B, S, 2 * DIFF_HEADS, HEAD_DIM), cos, sin)
    kb = apply_rope(kb.reshape(B, S, 2 * DIFF_HEADS, HEAD_DIM), cos, sin)
    qb = qb.reshape(B, S, DIFF_HEADS, 2, HEAD_DIM)
    kb = kb.reshape(B, S, DIFF_HEADS, 2, HEAD_DIM)
    vb = vb.reshape(B, S, DIFF_HEADS, 2 * HEAD_DIM)
    lam = (jnp.exp(jnp.sum(lq1.astype(jnp.float32) * lk1.astype(jnp.float32)))
           - jnp.exp(jnp.sum(lq2.astype(jnp.float32) * lk2.astype(jnp.float32)))
           + lambda_init)
    out_b = diff_attention(qb, kb, vb, lam, subln_w, lambda_init)
    return jnp.concatenate([out_a, out_b], axis=-1) @ w_out


def odd_mixer(h, w_in, w_out):
    B, S, _ = h.shape
    q, k, v = jnp.split(h @ w_in, 3, axis=-1)
    q = q.reshape(B, S, SB_HEADS, HEAD_DIM)
    k = k.reshape(B, S, SB_HEADS, HEAD_DIM)
    v = v.reshape(B, S, SB_HEADS, HEAD_DIM)
    o = stick_breaking_attention(q, k, v).reshape(B, S, D_MODEL)
    return o @ w_out


def moe_ffn(h, router_w, router_b, w_gu, b_gu, w_down, b_down):
    B, S, D = h.shape
    T = B * S
    TK = T * TOP_K
    xt = h.reshape(T, D)
    logits = (xt @ router_w + router_b).astype(jnp.float32)
    top_vals, top_idx = lax.top_k(logits, TOP_K)
    gates = jax.nn.softmax(top_vals, axis=-1)
    e_flat = top_idx.reshape(TK)
    tok_flat = jnp.arange(TK) // TOP_K
    order = jnp.argsort(e_flat)
    e_sorted = e_flat[order]
    tok_sorted = tok_flat[order]
    gate_sorted = gates.reshape(TK)[order]
    counts = jnp.bincount(e_flat, length=N_EXPERTS)
    padded = (counts + EXPERT_BLOCK - 1) // EXPERT_BLOCK * EXPERT_BLOCK
    pad_end = jnp.cumsum(padded)
    pad_start = pad_end - padded
    start = jnp.cumsum(counts) - counts
    dest = pad_start[e_sorted] + (jnp.arange(TK) - start[e_sorted])
    n_rows = -(-TK // EXPERT_BLOCK) * EXPERT_BLOCK + N_EXPERTS * EXPERT_BLOCK
    n_blocks = n_rows // EXPERT_BLOCK
    row_tok = jnp.zeros((n_rows,), jnp.int32).at[dest].set(tok_sorted)
    x_rows = xt[row_tok].reshape(n_blocks, EXPERT_BLOCK, D)
    blk_expert = jnp.minimum(
        jnp.searchsorted(pad_end, jnp.arange(n_blocks) * EXPERT_BLOCK, side='right'),
        N_EXPERTS - 1)

    def expert_block(args):
        xb, e = args
        gu = xb @ w_gu[e] + b_gu[e]
        g = jnp.minimum(gu[:, :D_EXPERT], SWIGLU_LIMIT)
        u = jnp.clip(gu[:, D_EXPERT:], -SWIGLU_LIMIT, SWIGLU_LIMIT)
        act = (u + 1.0) * (g * jax.nn.sigmoid(SWIGLU_ALPHA * g))
        return act @ w_down[e] + b_down[e]

    y_rows = lax.map(expert_block, (x_rows, blk_expert)).reshape(n_rows, D)
    contrib = y_rows[dest] * gate_sorted[:, None].astype(h.dtype)
    out = jnp.zeros((T, D), h.dtype).at[tok_sorted].add(contrib)
    return out.reshape(B, S, D)


def setup_inputs(seed: int = 0) -> dict:
    key = jax.random.key(seed)
    ks = jax.random.split(key, 24)
    f32 = jnp.float32
    D = D_MODEL

    def nrm(k, shape, scale):
        return jax.random.normal(k, shape, f32) * scale

    def gain(k, shape):
        return 1.0 + 0.05 * jax.random.normal(k, shape, f32)

    return {
        'x': nrm(ks[0], (BATCH, SEQ, D), 1.0),
        'c': nrm(ks[1], (BATCH, D), 1.0),
        'ada_down': nrm(ks[2], (DEPTH, D, ADA_RANK), D ** -0.5),
        'ada_up': nrm(ks[3], (DEPTH, ADA_RANK, 6 * D), 0.5 * ADA_RANK ** -0.5),
        'ada_b': nrm(ks[4], (DEPTH, 6 * D), 0.02),
        'norm_mix_pre': gain(ks[5], (DEPTH, D)),
        'norm_mix_post': gain(ks[6], (DEPTH, D)),
        'norm_ffn_pre': gain(ks[7], (DEPTH, D)),
        'norm_ffn_post': gain(ks[8], (DEPTH, D)),
        'w_in_even': nrm(ks[9], (N_EVEN, D, 3 * D), D ** -0.5),
        'w_out_even': nrm(ks[10], (N_EVEN, D, D), D ** -0.5),
        'diff_lambda_q1': nrm(ks[11], (N_EVEN, HEAD_DIM), 0.1),
        'diff_lambda_k1': nrm(ks[12], (N_EVEN, HEAD_DIM), 0.1),
        'diff_lambda_q2': nrm(ks[13], (N_EVEN, HEAD_DIM), 0.1),
        'diff_lambda_k2': nrm(ks[14], (N_EVEN, HEAD_DIM), 0.1),
        'diff_subln': gain(ks[15], (N_EVEN, 2 * HEAD_DIM)),
        'w_in_odd': nrm(ks[16], (N_ODD, D, 3 * D), D ** -0.5),
        'w_out_odd': nrm(ks[17], (N_ODD, D, D), D ** -0.5),
        'router_w': nrm(ks[18], (DEPTH, D, N_EXPERTS), D ** -0.5),
        'router_b': nrm(ks[19], (DEPTH, N_EXPERTS), 0.01),
        'expert_w_gu': nrm(ks[20], (DEPTH, N_EXPERTS, D, 2 * D_EXPERT), D ** -0.5),
        'expert_b_gu': nrm(ks[21], (DEPTH, N_EXPERTS, 2 * D_EXPERT), 0.02),
        'expert_w_down': nrm(ks[22], (DEPTH, N_EXPERTS, D_EXPERT, D), D_EXPERT ** -0.5),
        'expert_b_down': nrm(ks[23], (DEPTH, N_EXPERTS, D), 0.02),
    }


def reference(x, c, ada_down, ada_up, ada_b, norm_mix_pre, norm_mix_post, norm_ffn_pre,
              norm_ffn_post, w_in_even, w_out_even, diff_lambda_q1, diff_lambda_k1,
              diff_lambda_q2, diff_lambda_k2, diff_subln, w_in_odd, w_out_odd, router_w,
              router_b, expert_w_gu, expert_b_gu, expert_w_down, expert_b_down):
    S = x.shape[1]
    cos, sin = rope_tables(S)
    c_act = jax.nn.silu(c)
    for layer in range(DEPTH):
        mod = (c_act @ ada_down[layer]) @ ada_up[layer] + ada_b[layer]
        sh_m, sc_m, g_m, sh_f, sc_f, g_f = jnp.split(mod[:, None, :], 6, axis=-1)
        h = rmsnorm(x, norm_mix_pre[layer]) * (1.0 + sc_m) + sh_m
        if layer % 2 == 0:
            i = layer // 2
            lambda_init = 0.8 - 0.6 * float(np.exp(-0.3 * layer))
            y = even_mixer(h, w_in_even[i], w_out_even[i], diff_lambda_q1[i], diff_lambda_k1[i],
                           diff_lambda_q2[i], diff_lambda_k2[i], diff_subln[i], lambda_init,
                           cos, sin)
        else:
            j = layer // 2
            y = odd_mixer(h, w_in_odd[j], w_out_odd[j])
        x = x + g_m * rmsnorm(y, norm_mix_post[layer])
        h = rmsnorm(x, norm_ffn_pre[layer]) * (1.0 + sc_f) + sh_f
        y = moe_ffn(h, router_w[layer], router_b[layer], expert_w_gu[layer], expert_b_gu[layer],
                    expert_w_down[layer], expert_b_down[layer])
        x = x + g_f * rmsnorm(y, norm_ffn_post[layer])
    return x
```

```python
import functools

import numpy as np
import jax
import jax.numpy as jnp
from jax import lax
from jax.experimental import pallas as pl
from jax.experimental.pallas import tpu as pltpu

F32 = jnp.float32
BF16 = jnp.bfloat16
U32 = jnp.uint32

HEAD_DIM = 128
MOBA_BLOCK = 256
MOBA_TOPK = 3
ROPE_THETA = 10000.0
TOP_K = 4
SWIGLU_LIMIT = 7.0
SWIGLU_ALPHA = 1.702
EXPERT_BLOCK = 256
RMS_EPS = 1e-6
NEG_INF = -1e30
LANES = 128
ATTN_TILE = 256
V7X_VMEM_BYTES = 64 * 1024 * 1024
VMEM_LIMIT = V7X_VMEM_BYTES - 6 * 1024 * 1024
HIGHEST = lax.Precision.HIGHEST
NT_DIMS = (((1,), (1,)), ((), ()))


def _pick(dim, pref, mult=LANES):
    b = min(dim, pref) // mult * mult
    while dim % b:
        b -= mult
    return b


def _params(sem):
    return pltpu.CompilerParams(dimension_semantics=sem, vmem_limit_bytes=VMEM_LIMIT)


def _ada_kernel(c_ref, down_ref, up_ref, b_ref, o_ref):
    c = c_ref[...]
    c_act = c * jax.nn.sigmoid(c)
    t = jnp.dot(c_act, down_ref[0], preferred_element_type=F32, precision=HIGHEST)
    o_ref[0] = jnp.dot(t, up_ref[0], preferred_element_type=F32, precision=HIGHEST) + b_ref[0]


def _ada_mod(c, ada_down, ada_up, ada_b):
    depth, d, r = ada_down.shape
    n = ada_up.shape[2]
    b = c.shape[0]
    bp = -(-b // 8) * 8
    c_pad = jnp.pad(c, ((0, bp - b), (0, 0)))
    tn = _pick(n, 2048)
    out = pl.pallas_call(
        _ada_kernel,
        grid=(depth, n // tn),
        in_specs=[
            pl.BlockSpec((bp, d), lambda l, j: (0, 0)),
            pl.BlockSpec((1, d, r), lambda l, j: (l, 0, 0)),
            pl.BlockSpec((1, r, tn), lambda l, j: (l, 0, j)),
            pl.BlockSpec((1, 1, tn), lambda l, j: (l, 0, j)),
        ],
        out_specs=pl.BlockSpec((1, bp, tn), lambda l, j: (l, 0, j)),
        out_shape=jax.ShapeDtypeStruct((depth, bp, n), F32),
        compiler_params=_params(("parallel", "parallel")),
        name="ada_mod",
    )(c_pad, ada_down, ada_up, ada_b.reshape(depth, 1, n))
    return out[:, :b]


def _rms(v, w):
    return v * lax.rsqrt(jnp.mean(v * v, axis=-1, keepdims=True) + RMS_EPS) * w


def _pack_bf16_pair(h):
    half = h.shape[1] // 2
    hb = h.astype(BF16).astype(F32)
    lo = lax.bitcast_convert_type(hb[:, :half], U32) >> 16
    hi = lax.bitcast_convert_type(hb[:, half:], U32) & jnp.uint32(0xFFFF0000)
    return hi | lo


def _unpack_bf16_pair(w):
    lo = lax.bitcast_convert_type(w << 16, F32)
    hi = lax.bitcast_convert_type(w & jnp.uint32(0xFFFF0000), F32)
    return jnp.concatenate([lo, hi], axis=1).astype(BF16)


def _route(logits):
    lane = lax.broadcasted_iota(jnp.int32, logits.shape, 1).astype(F32)
    vals = logits
    tops, idxs = [], []
    for _ in range(TOP_K):
        m = jnp.max(vals, axis=-1, keepdims=True)
        idx = jnp.min(jnp.where(vals == m, lane, float(LANES)), axis=-1, keepdims=True)
        vals = jnp.where(lane == idx, -jnp.inf, vals)
        tops.append(m)
        idxs.append(idx)
    es = [jnp.exp(t - tops[0]) for t in tops]
    inv = 1.0 / (es[0] + es[1] + es[2] + es[3])
    idx_out = jnp.zeros(logits.shape, F32)
    gate_out = jnp.zeros(logits.shape, F32)
    for k in range(TOP_K):
        idx_out = jnp.where(lane == float(k), idxs[k], idx_out)
        gate_out = jnp.where(lane == float(k), es[k] * inv, gate_out)
    return idx_out.astype(jnp.int32), gate_out


def _norm_kernel(*refs, has_branch, has_next, has_router, packed):
    it = iter(refs)
    x = next(it)[...]
    if has_branch:
        y_ref, wpost_ref, g_ref = next(it), next(it), next(it)
        x = x + g_ref[0] * _rms(y_ref[...].astype(F32), wpost_ref[...])
    if has_next:
        wpre_ref, sc_ref, sh_ref = next(it), next(it), next(it)
    if has_router:
        rw_ref, rb_ref = next(it), next(it)
    if has_branch:
        next(it)[...] = x
    if has_next:
        h = _rms(x, wpre_ref[...]) * (1.0 + sc_ref[0]) + sh_ref[0]
        h_ref = next(it)
        h_ref[...] = _pack_bf16_pair(h) if packed else h.astype(h_ref.dtype)
        if has_router:
            logits = jnp.dot(h, rw_ref[...], preferred_element_type=F32, precision=HIGHEST) + rb_ref[...]
            idx, gates = _route(logits)
            next(it)[...] = idx
            next(it)[...] = gates


def _norm_call(x, seq, *, branch=None, nxt=None, router=None, packed=False, rows=256):
    t, d = x.shape
    tr = _pick(seq, rows, 8)
    per_batch = seq // tr
    row_spec = pl.BlockSpec((tr, d), lambda i: (i, 0))
    vec_spec = pl.BlockSpec((1, d), lambda i: (0, 0))
    mod_spec = pl.BlockSpec((1, 1, d), lambda i: (i // per_batch, 0, 0))
    args, in_specs, out_shape, out_specs = [x], [row_spec], [], []
    if branch is not None:
        y, wpost, g = branch
        args += [y, wpost.reshape(1, d), g]
        in_specs += [row_spec, vec_spec, mod_spec]
        out_shape.append(jax.ShapeDtypeStruct((t, d), F32))
        out_specs.append(row_spec)
    if nxt is not None:
        wpre, sc, sh = nxt
        args += [wpre.reshape(1, d), sc, sh]
        in_specs += [vec_spec, mod_spec, mod_spec]
        if router is not None:
            rw, rb = router
            e = rw.shape[1]
            args += [jnp.pad(rw, ((0, 0), (0, LANES - e))),
                     jnp.pad(rb.reshape(1, e), ((0, 0), (0, LANES - e)), constant_values=NEG_INF)]
            in_specs += [pl.BlockSpec((d, LANES), lambda i: (0, 0)), pl.BlockSpec((1, LANES), lambda i: (0, 0))]
        if packed:
            out_shape.append(jax.ShapeDtypeStruct((t, d // 2), U32))
            out_specs.append(pl.BlockSpec((tr, d // 2), lambda i: (i, 0)))
        else:
            out_shape.append(jax.ShapeDtypeStruct((t, d), BF16))
            out_specs.append(row_spec)
        if router is not None:
            out_shape += [jax.ShapeDtypeStruct((t, LANES), jnp.int32), jax.ShapeDtypeStruct((t, LANES), F32)]
            out_specs += [pl.BlockSpec((tr, LANES), lambda i: (i, 0))] * 2
    kern = functools.partial(_norm_kernel, has_branch=branch is not None, has_next=nxt is not None,
                             has_router=router is not None, packed=packed)
    return pl.pallas_call(
        kern, grid=(t // tr,), in_specs=in_specs, out_specs=out_specs, out_shape=out_shape,
        compiler_params=_params(("parallel",)), name="norm_mod",
    )(*args)


def _mm_kernel(*refs, n_parts, mode, scale, n_scaled):
    xs, ws = refs[:n_parts], refs[n_parts:2 * n_parts]
    o_ref = refs[-1]
    acc = jnp.dot(xs[0][...], ws[0][...], preferred_element_type=F32)
    for p in range(1, n_parts):
        acc = acc + jnp.dot(xs[p][...], ws[p][...], preferred_element_type=F32)
    if mode == "rope":
        a = refs[2 * n_parts][0]
        b = refs[2 * n_parts + 1][0]
        for h in range(o_ref.shape[1] // HEAD_DIM):
            t = acc[:, h * HEAD_DIM:(h + 1) * HEAD_DIM]
            rot = pltpu.roll(t, HEAD_DIM // 2, 1)
            o_ref[:, h * HEAD_DIM:(h + 1) * HEAD_DIM] = (t * a + rot * b).astype(o_ref.dtype)
    elif mode == "scale":
        f = jnp.where(pl.program_id(1) < n_scaled, scale, 1.0).astype(F32)
        o_ref[...] = (acc * f).astype(o_ref.dtype)
    else:
        o_ref[...] = acc.astype(o_ref.dtype)


def _matmul(xs, w, out_dtype, *, bn_cap, mode="plain", tables=None, seq=None, scale=1.0, n_scaled_cols=0):
    n_parts = len(xs)
    m = xs[0].shape[0]
    kp = xs[0].shape[1]
    n = w.shape[1]
    bm = _pick(m if seq is None else seq, 1024, 8)
    bn = _pick(bn_cap, 1024)
    args = list(xs) + [w] * n_parts
    in_specs = [pl.BlockSpec((bm, kp), lambda i, j: (i, 0)) for _ in range(n_parts)]
    in_specs += [pl.BlockSpec((kp, bn), functools.partial(lambda i, j, p: (p, j), p=p)) for p in range(n_parts)]
    if mode == "rope":
        ta, tb, n_regions = tables
        per_region = (n // n_regions) // bn
        per_seq = seq // bm
        tab_spec = pl.BlockSpec((1, bm, HEAD_DIM), lambda i, j: ((j // per_region) % 3, i % per_seq, 0))
        args += [ta, tb]
        in_specs += [tab_spec, tab_spec]
    kern = functools.partial(_mm_kernel, n_parts=n_parts, mode=mode, scale=scale, n_scaled=n_scaled_cols // bn)
    return pl.pallas_call(
        kern, grid=(m // bm, n // bn), in_specs=in_specs,
        out_specs=pl.BlockSpec((bm, bn), lambda i, j: (i, j)),
        out_shape=jax.ShapeDtypeStruct((m, n), out_dtype),
        compiler_params=_params(("parallel", "parallel")), name="matmul_" + mode,
    )(*args)


def _causal_mask(t):
    row = lax.broadcasted_iota(jnp.int32, (t, t), 0)
    col = lax.broadcasted_iota(jnp.int32, (t, t), 1)
    return row, col


def _kv_block(ref, j, t):
    return ref[0, pl.ds(pl.multiple_of(j * t, t), t), :]


def _moba_kernel(q_ref, k_ref, v_ref, o_ref, kmean_ref, bias_ref, *, nb):
    t = MOBA_BLOCK
    qi = pl.program_id(2)

    @pl.when(qi == 0)
    def _():
        kmean_ref[...] = jnp.zeros(kmean_ref.shape, F32)
        for j in range(nb):
            kj = k_ref[0, j * t:(j + 1) * t, :].astype(F32)
            kmean_ref[j:j + 1, :] = jnp.sum(kj, axis=0, keepdims=True) * (1.0 / t)

    q = q_ref[0]
    gate = lax.dot_general(q.astype(F32), kmean_ref[...], NT_DIMS, precision=HIGHEST,
                           preferred_element_type=F32)
    lane = lax.broadcasted_iota(jnp.int32, gate.shape, 1)
    lane_f = lane.astype(F32)
    past = lane < qi
    g = jnp.where(past, gate, NEG_INF)
    chosen = jnp.zeros(gate.shape, F32)
    for _ in range(MOBA_TOPK):
        m = jnp.max(g, axis=-1, keepdims=True)
        idx = jnp.min(jnp.where(g == m, lane_f, float(LANES)), axis=-1, keepdims=True)
        hit = lane_f == idx
        chosen = jnp.where(hit, 1.0, chosen)
        g = jnp.where(hit, -jnp.inf, g)
    bias = jnp.where(past, jnp.where(chosen > 0.0, 0.0, NEG_INF), NEG_INF)
    for j in range(nb):
        bias_ref[j] = jnp.broadcast_to(bias[:, j:j + 1], (t, LANES))

    row, col = _causal_mask(t)
    s = lax.dot_general(q, _kv_block(k_ref, qi, t), NT_DIMS, preferred_element_type=F32)
    s = jnp.where(col <= row, s, NEG_INF)
    m0 = jnp.max(s, axis=-1, keepdims=True)
    p = jnp.exp(s - m0)
    l0 = jnp.sum(p, axis=-1, keepdims=True)
    acc0 = jnp.dot(p.astype(BF16), _kv_block(v_ref, qi, t), preferred_element_type=F32)

    def body(j, carry):
        m_i, l_i, acc = carry
        b = bias_ref[j]
        s = lax.dot_general(q, _kv_block(k_ref, j, t), NT_DIMS, preferred_element_type=F32)
        s = s + jnp.concatenate([b, b], axis=1)
        m_new = jnp.maximum(m_i, jnp.max(s, axis=-1, keepdims=True))
        alpha = jnp.exp(m_i - m_new)
        p = jnp.exp(s - m_new)
        l_new = alpha * l_i + jnp.sum(p, axis=-1, keepdims=True)
        acc = alpha * acc + jnp.dot(p.astype(BF16), _kv_block(v_ref, j, t), preferred_element_type=F32)
        return m_new, l_new, acc

    _, l_i, acc = lax.fori_loop(0, qi, body, (m0, l0, acc0))
    o_ref[0] = (acc * (1.0 / l_i)).astype(o_ref.dtype)


def _moba_call(proj, n_heads):
    b, s, _ = proj.shape
    t = MOBA_BLOCK
    nb = s // t
    assert s % t == 0 and nb <= LANES
    nbp = -(-nb // 8) * 8
    return pl.pallas_call(
        functools.partial(_moba_kernel, nb=nb),
        grid=(b, n_heads, nb),
        in_specs=[
            pl.BlockSpec((1, t, HEAD_DIM), lambda bi, h, i: (bi, i, h)),
            pl.BlockSpec((1, s, HEAD_DIM), lambda bi, h, i: (bi, 0, n_heads + h)),
            pl.BlockSpec((1, s, HEAD_DIM), lambda bi, h, i: (bi, 0, 2 * n_heads + h)),
        ],
        out_specs=pl.BlockSpec((1, t, HEAD_DIM), lambda bi, h, i: (bi, i, h)),
        out_shape=jax.ShapeDtypeStruct((b, s, n_heads * HEAD_DIM), BF16),
        scratch_shapes=[pltpu.VMEM((max(nbp, LANES), HEAD_DIM), F32), pltpu.VMEM((nb, t, LANES), F32)],
        compiler_params=_params(("arbitrary", "arbitrary", "arbitrary")), name="moba_attn",
    )(proj, proj, proj)


def _diff_kernel(q_ref, k_ref, v_ref, lq1_ref, lk1_ref, lq2_ref, lk2_ref, sub_ref, o_ref, acc_ref, *, lambda_init):
    t = ATTN_TILE
    qi = pl.program_id(2)
    q = q_ref[0]
    row, col = _causal_mask(t)
    acc_ref[...] = jnp.zeros(acc_ref.shape, F32)

    def step(j, carry, masked):
        kj = _kv_block(k_ref, j, t)
        vj = _kv_block(v_ref, j, t)
        out = []
        for c in range(2):
            m_i, l_i = carry[2 * c], carry[2 * c + 1]
            sl = slice(c * HEAD_DIM, (c + 1) * HEAD_DIM)
            s = lax.dot_general(q[:, sl], kj[:, sl], NT_DIMS, preferred_element_type=F32)
            if masked:
                s = jnp.where(col <= row, s, NEG_INF)
            m_new = jnp.maximum(m_i, jnp.max(s, axis=-1, keepdims=True))
            alpha = jnp.exp(m_i - m_new)
            p = jnp.exp(s - m_new)
            l_new = alpha * l_i + jnp.sum(p, axis=-1, keepdims=True)
            acc_ref[c] = alpha * acc_ref[c] + jnp.dot(p.astype(BF16), vj, preferred_element_type=F32)
            out += [m_new, l_new]
        return tuple(out)

    neg = jnp.full((t, 1), -jnp.inf, F32)
    zero = jnp.zeros((t, 1), F32)
    carry = step(qi, (neg, zero, neg, zero), True)
    carry = lax.fori_loop(0, qi, lambda j, cr: step(j, cr, False), carry)
    lam = (jnp.exp(jnp.sum(lq1_ref[...] * lk1_ref[...], axis=-1, keepdims=True))
           - jnp.exp(jnp.sum(lq2_ref[...] * lk2_ref[...], axis=-1, keepdims=True)) + lambda_init)
    o = acc_ref[0] * (1.0 / carry[1]) - lam * (acc_ref[1] * (1.0 / carry[3]))
    o_ref[0] = (_rms(o, sub_ref[...]) * (1.0 - lambda_init)).astype(o_ref.dtype)


def _diff_call(proj, n_heads, col0, lq1, lk1, lq2, lk2, subln, lambda_init):
    b, s, _ = proj.shape
    t = ATTN_TILE
    w = 2 * HEAD_DIM
    vec = lambda a: a.reshape(1, -1).astype(F32)
    vspec = pl.BlockSpec((1, HEAD_DIM), lambda bi, h, i: (0, 0))
    return pl.pallas_call(
        functools.partial(_diff_kernel, lambda_init=lambda_init),
        grid=(b, n_heads, s // t),
        in_specs=[
            pl.BlockSpec((1, t, w), lambda bi, h, i: (bi, i, col0 + h)),
            pl.BlockSpec((1, s, w), lambda bi, h, i: (bi, 0, col0 + n_heads + h)),
            pl.BlockSpec((1, s, w), lambda bi, h, i: (bi, 0, col0 + 2 * n_heads + h)),
            vspec, vspec, vspec, vspec,
            pl.BlockSpec((1, w), lambda bi, h, i: (0, 0)),
        ],
        out_specs=pl.BlockSpec((1, t, w), lambda bi, h, i: (bi, i, h)),
        out_shape=jax.ShapeDtypeStruct((b, s, n_heads * w), BF16),
        scratch_shapes=[pltpu.VMEM((2, t, w), F32)],
        compiler_params=_params(("parallel", "parallel", "parallel")), name="diff_attn",
    )(proj, proj, proj, vec(lq1), vec(lk1), vec(lq2), vec(lk2), vec(subln))


def _sb_kernel(q_ref, k_ref, v_ref, u_ref, o_ref):
    t = ATTN_TILE
    qi = pl.program_id(2)
    q = q_ref[0]
    u = u_ref[...]
    row, col = _causal_mask(t)
    strict = col < row

    def tile(j, c, masked):
        z = lax.dot_general(q, _kv_block(k_ref, j, t), NT_DIMS, preferred_element_type=F32)
        soft = jnp.log(1.0 + jnp.exp(-jnp.abs(z)))
        log_beta = jnp.minimum(z, 0.0) - soft
        log_1m = jnp.minimum(-z, 0.0) - soft
        if masked:
            log_1m = jnp.where(strict, log_1m, 0.0)
        hi = log_1m.astype(BF16)
        lo = (log_1m - hi.astype(F32)).astype(BF16)
        tail = jnp.dot(hi, u, preferred_element_type=F32) + jnp.dot(lo, u, preferred_element_type=F32)
        a = jnp.exp(log_beta + tail + c)
        if masked:
            a = jnp.where(strict, a, 0.0)
        pv = jnp.dot(a.astype(BF16), _kv_block(v_ref, j, t), preferred_element_type=F32)
        total = tail[:, 0:1] + log_1m[:, 0:1]
        return pv, c + total

    acc, c = tile(qi, jnp.zeros((t, 1), F32), True)

    def body(n, carry):
        acc, c = carry
        pv, c = tile(qi - 1 - n, c, False)
        return acc + pv, c

    acc, _ = lax.fori_loop(0, qi, body, (acc, c))
    o_ref[0] = acc.astype(o_ref.dtype)


def _sb_call(proj, n_heads):
    b, s, _ = proj.shape
    t = ATTN_TILE
    idx = np.arange(t)
    u = jnp.asarray(idx[:, None] > idx[None, :], BF16)
    return pl.pallas_call(
        _sb_kernel,
        grid=(b, n_heads, s // t),
        in_specs=[
            pl.BlockSpec((1, t, HEAD_DIM), lambda bi, h, i: (bi, i, h)),
            pl.BlockSpec((1, s, HEAD_DIM), lambda bi, h, i: (bi, 0, n_heads + h)),
            pl.BlockSpec((1, s, HEAD_DIM), lambda bi, h, i: (bi, 0, 2 * n_heads + h)),
            pl.BlockSpec((t, t), lambda bi, h, i: (0, 0)),
        ],
        out_specs=pl.BlockSpec((1, t, HEAD_DIM), lambda bi, h, i: (bi, i, h)),
        out_shape=jax.ShapeDtypeStruct((b, s, n_heads * HEAD_DIM), BF16),
        compiler_params=_params(("parallel", "parallel", "parallel")), name="sb_attn",
    )(proj, proj, proj, u)


def _gather_rows(idx_ref, n, src_hbm, dst, sem):
    def body(r, _):
        pltpu.make_async_copy(src_hbm.at[pl.ds(idx_ref[0, 0, r], 1)], dst.at[pl.ds(r, 1)], sem).start()
        return 0
    lax.fori_loop(0, n, body, 0)


def _wait_rows(n, src_hbm, dst, sem):
    def body(r, _):
        pltpu.make_async_copy(src_hbm.at[pl.ds(0, 1)], dst.at[pl.ds(r, 1)], sem).wait()
        return 0
    lax.fori_loop(0, n, body, 0)


def _expert_kernel(blk_expert_ref, n_active_ref, tok_ref, tok_next_ref, h_hbm, wgu_ref, bgu_ref, wd_ref, bd_ref,
                   o_ref, xbuf, sem, *, f):
    i = pl.program_id(0)
    n_active = n_active_ref[0]
    slot = i % 2
    rows = xbuf.shape[1]

    @pl.when(i == 0)
    def _():
        _gather_rows(tok_ref, rows, h_hbm, xbuf.at[0], sem.at[0])

    @pl.when(i + 1 < n_active)
    def _():
        _gather_rows(tok_next_ref, rows, h_hbm, xbuf.at[1 - slot], sem.at[1 - slot])

    @pl.when(i < n_active)
    def _():
        _wait_rows(rows, h_hbm, xbuf.at[slot], sem.at[slot])
        x = _unpack_bf16_pair(xbuf[slot])
        gu = jnp.dot(x, wgu_ref[0], preferred_element_type=F32) + bgu_ref[0]
        g = jnp.minimum(gu[:, :f], SWIGLU_LIMIT)
        u = jnp.clip(gu[:, f:], -SWIGLU_LIMIT, SWIGLU_LIMIT)
        act = (u + 1.0) * (g * jax.nn.sigmoid(SWIGLU_ALPHA * g))
        o_ref[...] = jnp.dot(act.astype(BF16), wd_ref[0], preferred_element_type=F32) + bd_ref[0]

    @pl.when(i >= n_active)
    def _():
        o_ref[...] = jnp.zeros(o_ref.shape, o_ref.dtype)


def _expert_call(h_packed, blk_expert, n_active, row_tok, w_gu, b_gu, w_down, b_down):
    n_blocks = blk_expert.shape[0]
    rows = EXPERT_BLOCK
    e, d, f2 = w_gu.shape
    f = f2 // 2
    half = h_packed.shape[1]
    tok3 = row_tok.reshape(n_blocks, 1, rows)
    last = n_blocks - 1
    active = lambda i, be, na: jnp.minimum(i, na[0] - 1)
    smem_blk = lambda imap: pl.BlockSpec((1, 1, rows), imap, memory_space=pltpu.SMEM)
    grid_spec = pltpu.PrefetchScalarGridSpec(
        num_scalar_prefetch=2,
        grid=(n_blocks,),
        in_specs=[
            smem_blk(lambda i, be, na: (i, 0, 0)),
            smem_blk(lambda i, be, na: (jnp.minimum(i + 1, last), 0, 0)),
            pl.BlockSpec(memory_space=pl.ANY),
            pl.BlockSpec((1, d, f2), lambda i, be, na: (be[active(i, be, na)], 0, 0)),
            pl.BlockSpec((1, 1, f2), lambda i, be, na: (be[active(i, be, na)], 0, 0)),
            pl.BlockSpec((1, f, d), lambda i, be, na: (be[active(i, be, na)], 0, 0)),
            pl.BlockSpec((1, 1, d), lambda i, be, na: (be[active(i, be, na)], 0, 0)),
        ],
        out_specs=pl.BlockSpec((rows, d), lambda i, be, na: (i, 0)),
        scratch_shapes=[pltpu.VMEM((2, rows, half), U32), pltpu.SemaphoreType.DMA((2,))],
    )
    return pl.pallas_call(
        functools.partial(_expert_kernel, f=f),
        grid_spec=grid_spec,
        out_shape=jax.ShapeDtypeStruct((n_blocks * rows, d), F32),
        compiler_params=_params(("arbitrary",)), name="moe_experts",
    )(blk_expert, n_active, tok3, tok3, h_packed, w_gu, b_gu.reshape(e, 1, f2), w_down, b_down.reshape(e, 1, d))


def _combine_kernel(*refs, has_next, tr):
    pos_ref, pos_next_ref, yrows_hbm, gates_ref, x_ref, wpost_ref, g_ref = refs[:7]
    rest = refs[7:]
    if has_next:
        wpre_ref, sc_ref, sh_ref = rest[:3]
        rest = rest[3:]
        xo_ref, h_ref, ybuf, sem = rest
    else:
        xo_ref, ybuf, sem = rest
    i = pl.program_id(0)
    n = pl.num_programs(0)
    slot = i % 2
    cnt = TOP_K * tr

    @pl.when(i == 0)
    def _():
        _gather_rows(pos_ref, cnt, yrows_hbm, ybuf.at[0], sem.at[0])

    @pl.when(i + 1 < n)
    def _():
        _gather_rows(pos_next_ref, cnt, yrows_hbm, ybuf.at[1 - slot], sem.at[1 - slot])

    _wait_rows(cnt, yrows_hbm, ybuf.at[slot], sem.at[slot])
    gates = gates_ref[...]
    y = gates[:, 0:1] * ybuf[slot, 0:tr, :]
    for k in range(1, TOP_K):
        y = y + gates[:, k:k + 1] * ybuf[slot, k * tr:(k + 1) * tr, :]
    x = x_ref[...] + g_ref[0] * _rms(y, wpost_ref[...])
    xo_ref[...] = x
    if has_next:
        h = _rms(x, wpre_ref[...]) * (1.0 + sc_ref[0]) + sh_ref[0]
        h_ref[...] = h.astype(h_ref.dtype)


def _combine_call(y_rows, pos, gates, x, seq, wpost, g, nxt=None, rows=128):
    t, d = x.shape
    tr = _pick(seq, rows, 8)
    per_batch = seq // tr
    n = t // tr
    pos3 = pos.reshape(n, tr, TOP_K).transpose(0, 2, 1).reshape(n, 1, TOP_K * tr)
    row_spec = pl.BlockSpec((tr, d), lambda i: (i, 0))
    vec_spec = pl.BlockSpec((1, d), lambda i: (0, 0))
    mod_spec = pl.BlockSpec((1, 1, d), lambda i: (i // per_batch, 0, 0))
    smem_blk = lambda imap: pl.BlockSpec((1, 1, TOP_K * tr), imap, memory_space=pltpu.SMEM)
    args = [pos3, pos3, y_rows, gates, x, wpost.reshape(1, d), g]
    in_specs = [smem_blk(lambda i: (i, 0, 0)), smem_blk(lambda i: (jnp.minimum(i + 1, n - 1), 0, 0)),
                pl.BlockSpec(memory_space=pl.ANY), pl.BlockSpec((tr, LANES), lambda i: (i, 0)),
                row_spec, vec_spec, mod_spec]
    out_shape = [jax.ShapeDtypeStruct((t, d), F32)]
    out_specs = [row_spec]
    if nxt is not None:
        wpre, sc, sh = nxt
        args += [wpre.reshape(1, d), sc, sh]
        in_specs += [vec_spec, mod_spec, mod_spec]
        out_shape.append(jax.ShapeDtypeStruct((t, d), BF16))
        out_specs.append(row_spec)
    return pl.pallas_call(
        functools.partial(_combine_kernel, has_next=nxt is not None, tr=tr),
        grid=(n,), in_specs=in_specs, out_specs=out_specs, out_shape=out_shape,
        scratch_shapes=[pltpu.VMEM((2, TOP_K * tr, d), F32), pltpu.SemaphoreType.DMA((2,))],
        compiler_params=_params(("arbitrary",)), name="moe_combine",
    )(*args)


def _dispatch_plan(top_idx, n_experts):
    t = top_idx.shape[0]
    tk = t * TOP_K
    e_flat = top_idx.reshape(tk)
    onehot = (e_flat[:, None] == jnp.arange(n_experts, dtype=jnp.int32)[None, :]).astype(jnp.int32)
    csum = jnp.cumsum(onehot, axis=0)
    rank = jnp.sum(csum * onehot, axis=1) - 1
    counts = csum[-1]
    padded = (counts + EXPERT_BLOCK - 1) // EXPERT_BLOCK * EXPERT_BLOCK
    pad_end = jnp.cumsum(padded)
    pad_start = pad_end - padded
    pos = jnp.sum(onehot * pad_start[None, :], axis=1) + rank
    n_blocks = -(-tk // EXPERT_BLOCK) + n_experts
    row_tok = jnp.zeros((n_blocks * EXPERT_BLOCK,), jnp.int32).at[pos].set(jnp.arange(tk, dtype=jnp.int32) // TOP_K)
    n_active = (pad_end[-1] // EXPERT_BLOCK).astype(jnp.int32)
    blk_start = jnp.minimum(jnp.arange(n_blocks, dtype=jnp.int32), n_active - 1) * EXPERT_BLOCK
    blk_expert = jnp.sum((pad_end[None, :] <= blk_start[:, None]).astype(jnp.int32), axis=1)
    blk_expert = jnp.minimum(blk_expert, n_experts - 1).astype(jnp.int32)
    return pos.reshape(t, TOP_K).astype(jnp.int32), row_tok, blk_expert, n_active.reshape(1)


def _rope_tables(seq, scale):
    pos = jnp.arange(seq, dtype=F32)
    inv_freq = ROPE_THETA ** (-jnp.arange(0, HEAD_DIM, 2, dtype=F32) / HEAD_DIM)
    ang = pos[:, None] * inv_freq[None, :]
    cos, sin = jnp.cos(ang), jnp.sin(ang)
    a = jnp.concatenate([cos, cos], axis=-1)
    b = jnp.concatenate([-sin, sin], axis=-1)
    ta = jnp.stack([a * scale, a, jnp.ones_like(a)])
    tb = jnp.stack([b * scale, b, jnp.zeros_like(b)])
    return ta, tb


def kernel(x, c, ada_down, ada_up, ada_b, norm_mix_pre, norm_mix_post, norm_ffn_pre, norm_ffn_post, w_in_even, w_out_even, diff_lambda_q1, diff_lambda_k1, diff_lambda_q2, diff_lambda_k2, diff_subln, w_in_odd, w_out_odd, router_w, router_b, expert_w_gu, expert_b_gu, expert_w_down, expert_b_down):
    bsz, seq, d = x.shape
    depth = ada_down.shape[0]
    t = bsz * seq
    n_experts = router_w.shape[2]
    n_heads = d // HEAD_DIM
    scale = HEAD_DIM ** -0.5

    mod = _ada_mod(c, ada_down, ada_up, ada_b).reshape(depth, bsz, 6, 1, d)
    mods = [[mod[l, :, k] for k in range(6)] for l in range(depth)]
    ta, tb = _rope_tables(seq, scale)

    xf = x.reshape(t, d)
    sh_m, sc_m = mods[0][0], mods[0][1]
    (h,) = _norm_call(xf, seq, nxt=(norm_mix_pre[0], sc_m, sh_m))
    for layer in range(depth):
        sh_m, sc_m, g_m, sh_f, sc_f, g_f = mods[layer]
        if layer % 2 == 0:
            i = layer // 2
            lambda_init = 0.8 - 0.6 * float(np.exp(-0.3 * layer))
            proj = _matmul([h], w_in_even[i].astype(BF16), BF16, bn_cap=d // 2, mode="rope",
                           tables=(ta, tb, 6), seq=seq).reshape(bsz, seq, 3 * d)
            half_heads = n_heads // 2
            out_a = _moba_call(proj, half_heads)
            out_b = _diff_call(proj, half_heads // 2, 3 * (half_heads // 2), diff_lambda_q1[i], diff_lambda_k1[i],
                               diff_lambda_q2[i], diff_lambda_k2[i], diff_subln[i], lambda_init)
            y = _matmul([out_a.reshape(t, d // 2), out_b.reshape(t, d // 2)], w_out_even[i].astype(BF16), F32,
                        bn_cap=d, seq=seq)
        else:
            j = layer // 2
            proj = _matmul([h], w_in_odd[j].astype(BF16), BF16, bn_cap=d, mode="scale", seq=seq,
                           scale=scale, n_scaled_cols=d).reshape(bsz, seq, 3 * d)
            out = _sb_call(proj, n_heads)
            y = _matmul([out.reshape(t, d)], w_out_odd[j].astype(BF16), F32, bn_cap=d, seq=seq)
        xf, h_packed, top_idx, gates = _norm_call(
            xf, seq, branch=(y, norm_mix_post[layer], g_m), nxt=(norm_ffn_pre[layer], sc_f, sh_f),
            router=(router_w[layer], router_b[layer]), packed=True)
        pos, row_tok, blk_expert, n_active = _dispatch_plan(top_idx[:, :TOP_K], n_experts)
        y_rows = _expert_call(h_packed, blk_expert, n_active, row_tok, expert_w_gu[layer].astype(BF16),
                              expert_b_gu[layer], expert_w_down[layer].astype(BF16), expert_b_down[layer])
        if layer + 1 < depth:
            nsh, nsc = mods[layer + 1][0], mods[layer + 1][1]
            xf, h = _combine_call(y_rows, pos, gates, xf, seq, norm_ffn_post[layer], g_f,
                                  nxt=(norm_mix_pre[layer + 1], nsc, nsh))
        else:
            (xf,) = _combine_call(y_rows, pos, gates, xf, seq, norm_ffn_post[layer], g_f)
    return xf.reshape(bsz, seq, d)
```
